```python
import jax, jax.numpy as jnp
from jax import lax
import numpy as np

D_MODEL = 1024
BATCH = 1
SEQ = 16384
DEPTH = 4

N_MIXERS = 2
N_LAYERS_A = (DEPTH + N_MIXERS - 1) // N_MIXERS
N_LAYERS_B = DEPTH // N_MIXERS
D_FF = 2816
FFN_RESIDUAL_SCALE = 0.5
N_SUBLAYER_NORMS = 6
SGU_DIM = 2 * D_MODEL
SGU_GROUPS = 8
SGU_GROUP_DIM = SGU_DIM // SGU_GROUPS
CHUNK = 128
CONV_DIM = D_MODEL
CONV_WIDTH = 31
EPS = 1e-6

kernel_name = "hybrid_sgu_conformer_conv_macaron"


def rms_norm(x, g):
    xf = x.astype(jnp.float32)
    y = xf * lax.rsqrt(jnp.mean(xf * xf, axis=-1, keepdims=True) + EPS)
    return (y * g.astype(jnp.float32)).astype(x.dtype)


def layer_norm(x, g, b):
    xf = x.astype(jnp.float32)
    mu = jnp.mean(xf, axis=-1, keepdims=True)
    xc = xf - mu
    var = jnp.mean(xc * xc, axis=-1, keepdims=True)
    y = xc * lax.rsqrt(var + EPS) * g.astype(jnp.float32) + b.astype(jnp.float32)
    return y.astype(x.dtype)


def swiglu(h, w_gate, w_up, w_down):
    return (jax.nn.silu(h @ w_gate) * (h @ w_up)) @ w_down


def sgu_mixer(h, w_in, ln_g, ln_b, w_spatial, b_spatial, w_out):
    bsz, seq, _ = h.shape
    z = jax.nn.gelu(h @ w_in)
    u, v = jnp.split(z, 2, axis=-1)
    v = layer_norm(v, ln_g, ln_b)
    vc = v.reshape(bsz, seq // CHUNK, CHUNK, SGU_GROUPS, SGU_GROUP_DIM)
    causal = jnp.tril(jnp.ones((CHUNK, CHUNK), dtype=bool))
    ws = jnp.where(causal[None], w_spatial, 0)
    mixed = jnp.einsum('gts,bcsgd->bctgd', ws, vc)
    mixed = mixed + b_spatial.T[None, None, :, :, None]
    gated = u * mixed.reshape(bsz, seq, SGU_DIM)
    return gated @ w_out


def conv_mixer(h, w_pw1, w_dw, b_dw, ln_g, ln_b, w_pw2):
    a, gate = jnp.split(h @ w_pw1, 2, axis=-1)
    y = a * jax.nn.sigmoid(gate)
    y = lax.conv_general_dilated(
        y, w_dw[:, None, :].astype(y.dtype),
        window_strides=(1,),
        padding=[(CONV_WIDTH - 1, 0)],
        dimension_numbers=('NWC', 'WIO', 'NWC'),
        feature_group_count=CONV_DIM) + b_dw
    y = jax.nn.silu(layer_norm(y, ln_g, ln_b))
    return y @ w_pw2


def setup_inputs(seed: int = 0) -> dict:
    key = jax.random.key(seed)
    ks = jax.random.split(key, 18)
    f32 = jnp.float32
    nrm = lambda k, shape, scale: (jax.random.normal(k, shape, f32) * scale).astype(f32)
    x = jax.random.normal(ks[0], (BATCH, SEQ, D_MODEL), f32)
    norm_g = 1.0 + nrm(ks[1], (DEPTH, N_SUBLAYER_NORMS, D_MODEL), 0.05)
    ff_w_gate = nrm(ks[2], (DEPTH, 2, D_MODEL, D_FF), D_MODEL ** -0.5)
    ff_w_up = nrm(ks[3], (DEPTH, 2, D_MODEL, D_FF), D_MODEL ** -0.5)
    ff_w_down = nrm(ks[4], (DEPTH, 2, D_FF, D_MODEL), D_FF ** -0.5)
    sgu_w_in = nrm(ks[5], (N_LAYERS_A, D_MODEL, 2 * SGU_DIM), D_MODEL ** -0.5)
    sgu_ln_g = 1.0 + nrm(ks[6], (N_LAYERS_A, SGU_DIM), 0.05)
    sgu_ln_b = nrm(ks[7], (N_LAYERS_A, SGU_DIM), 0.02)
    sgu_w_spatial = nrm(ks[8], (N_LAYERS_A, SGU_GROUPS, CHUNK, CHUNK), CHUNK ** -0.5)
    sgu_b_spatial = 1.0 + nrm(ks[9], (N_LAYERS_A, SGU_GROUPS, CHUNK), 0.1)
    sgu_w_out = nrm(ks[10], (N_LAYERS_A, SGU_DIM, D_MODEL), SGU_DIM ** -0.5)
    conv_w_pw1 = nrm(ks[11], (N_LAYERS_B, D_MODEL, 2 * CONV_DIM), D_MODEL ** -0.5)
    conv_w_dw = nrm(ks[12], (N_LAYERS_B, CONV_WIDTH, CONV_DIM), CONV_WIDTH ** -0.5)
    conv_b_dw = nrm(ks[13], (N_LAYERS_B, CONV_DIM), 0.02)
    conv_ln_g = 1.0 + nrm(ks[14], (N_LAYERS_B, CONV_DIM), 0.05)
    conv_ln_b = nrm(ks[15], (N_LAYERS_B, CONV_DIM), 0.02)
    conv_w_pw2 = nrm(ks[16], (N_LAYERS_B, CONV_DIM, D_MODEL), CONV_DIM ** -0.5)
    return {
        "x": x, "norm_g": norm_g,
        "ff_w_gate": ff_w_gate, "ff_w_up": ff_w_up, "ff_w_down": ff_w_down,
        "sgu_w_in": sgu_w_in, "sgu_ln_g": sgu_ln_g, "sgu_ln_b": sgu_ln_b,
        "sgu_w_spatial": sgu_w_spatial, "sgu_b_spatial": sgu_b_spatial, "sgu_w_out": sgu_w_out,
        "conv_w_pw1": conv_w_pw1, "conv_w_dw": conv_w_dw, "conv_b_dw": conv_b_dw,
        "conv_ln_g": conv_ln_g, "conv_ln_b": conv_ln_b, "conv_w_pw2": conv_w_pw2,
    }


def reference(x, norm_g, ff_w_gate, ff_w_up, ff_w_down,
              sgu_w_in, sgu_ln_g, sgu_ln_b, sgu_w_spatial, sgu_b_spatial, sgu_w_out,
              conv_w_pw1, conv_w_dw, conv_b_dw, conv_ln_g, conv_ln_b, conv_w_pw2):
    for i in range(DEPTH):
        g = norm_g[i]
        h = swiglu(rms_norm(x, g[0]), ff_w_gate[i, 0], ff_w_up[i, 0], ff_w_down[i, 0])
        x = x + FFN_RESIDUAL_SCALE * rms_norm(h, g[1])
        hn = rms_norm(x, g[2])
        j = i // N_MIXERS
        if i % N_MIXERS == 0:
            m = sgu_mixer(hn, sgu_w_in[j], sgu_ln_g[j], sgu_ln_b[j],
                          sgu_w_spatial[j], sgu_b_spatial[j], sgu_w_out[j])
        else:
            m = conv_mixer(hn, conv_w_pw1[j], conv_w_dw[j], conv_b_dw[j],
                           conv_ln_g[j], conv_ln_b[j], conv_w_pw2[j])
        x = x + rms_norm(m, g[3])
        h = swiglu(rms_norm(x, g[4]), ff_w_gate[i, 1], ff_w_up[i, 1], ff_w_down[i, 1])
        x = x + FFN_RESIDUAL_SCALE * rms_norm(h, g[5])
    return x
```

```python
import functools

import jax
import jax.numpy as jnp
from jax import lax
from jax.experimental import pallas as pl
from jax.experimental.pallas import tpu as pltpu

EPS = 1e-6
FFN_RESIDUAL_SCALE = 0.5
SGU_GROUPS = 8
CHUNK = 128
CONV_WIDTH = 31
CONV_HALO = 32

ROW_TILE = 512
FF_COL_CHUNK = 256
SGU_COL_CHUNK = 512
CONV_ROW_BLOCK = 32
VMEM_LIMIT_BYTES = 56 * 1024 * 1024

F32 = jnp.float32
BF16 = jnp.bfloat16


def _rms(x, g):
    return x * lax.rsqrt(jnp.mean(x * x, axis=-1, keepdims=True) + EPS) * g


def _layer_norm(x, g, b):
    mu = jnp.mean(x, axis=-1, keepdims=True)
    xc = x - mu
    var = jnp.mean(xc * xc, axis=-1, keepdims=True)
    return xc * lax.rsqrt(var + EPS) * g + b


def _dot(a, b):
    return jnp.dot(a, b, preferred_element_type=F32)


def _ffn_kernel(x_ref, gpre_ref, gpost_ref, wg_ref, wu_ref, wd_ref, o_ref, h_ref, act_ref):
    d_ff = wg_ref.shape[1]
    h_ref[...] = _rms(x_ref[...], gpre_ref[...]).astype(BF16)
    for j in range(d_ff // FF_COL_CHUNK):
        sl = slice(j * FF_COL_CHUNK, (j + 1) * FF_COL_CHUNK)
        h = h_ref[...]
        gate = _dot(h, wg_ref[:, sl])
        up = _dot(h, wu_ref[:, sl])
        act_ref[:, sl] = (jax.nn.silu(gate) * up).astype(BF16)
    y = _dot(act_ref[...], wd_ref[...])
    o_ref[...] = x_ref[...] + FFN_RESIDUAL_SCALE * _rms(y, gpost_ref[...])


def _resident(shape, index_map):
    return pl.BlockSpec(shape, index_map, pipeline_mode=pl.Buffered(1))


def _ffn(x, norm_g, wg, wu, wd, layer, half):
    seq, d = x.shape
    d_ff = wg.shape[-1]
    tm = ROW_TILE
    assert seq % tm == 0 and d_ff % FF_COL_CHUNK == 0
    g_pre, g_post = (0, 1) if half == 0 else (4, 5)
    return pl.pallas_call(
        _ffn_kernel,
        grid=(seq // tm,),
        in_specs=[
            pl.BlockSpec((tm, d), lambda i: (i, 0)),
            _resident((None, None, 1, d), lambda i: (layer, g_pre, 0, 0)),
            _resident((None, None, 1, d), lambda i: (layer, g_post, 0, 0)),
            _resident((None, None, d, d_ff), lambda i: (layer, half, 0, 0)),
            _resident((None, None, d, d_ff), lambda i: (layer, half, 0, 0)),
            _resident((None, None, d_ff, d), lambda i: (layer, half, 0, 0)),
        ],
        out_specs=pl.BlockSpec((tm, d), lambda i: (i, 0)),
        out_shape=jax.ShapeDtypeStruct((seq, d), F32),
        scratch_shapes=[pltpu.VMEM((tm, d), BF16), pltpu.VMEM((tm, d_ff), BF16)],
        compiler_params=pltpu.CompilerParams(
            dimension_semantics=("arbitrary",), vmem_limit_bytes=VMEM_LIMIT_BYTES),
        name=f"ffn_l{layer}_h{half}",
    )(x, norm_g, norm_g, wg, wu, wd)


def _sgu_kernel(x_ref, gpre_ref, gpost_ref, win_ref, lng_ref, lnb_ref, ws_ref, bs_ref, wout_ref,
                o_ref, h_ref, u_ref, v_ref, gated_ref):
    tm = x_ref.shape[0]
    sgu_dim = u_ref.shape[1]
    group_dim = sgu_dim // SGU_GROUPS
    h_ref[...] = _rms(x_ref[...], gpre_ref[...]).astype(BF16)
    for j in range(sgu_dim // SGU_COL_CHUNK):
        sl = slice(j * SGU_COL_CHUNK, (j + 1) * SGU_COL_CHUNK)
        slv = slice(sgu_dim + j * SGU_COL_CHUNK, sgu_dim + (j + 1) * SGU_COL_CHUNK)
        h = h_ref[...]
        u_ref[:, sl] = jax.nn.gelu(_dot(h, win_ref[:, sl]))
        v_ref[:, sl] = jax.nn.gelu(_dot(h, win_ref[:, slv]))
    v_ref[...] = _layer_norm(v_ref[...], lng_ref[...], lnb_ref[...])
    row = lax.broadcasted_iota(jnp.int32, (CHUNK, CHUNK), 0)
    col = lax.broadcasted_iota(jnp.int32, (CHUNK, CHUNK), 1)
    causal = row >= col
    for g in range(SGU_GROUPS):
        gsl = slice(g * group_dim, (g + 1) * group_dim)
        ws = jnp.where(causal, ws_ref[g], 0.0).astype(BF16)
        bias = bs_ref[:, g:g + 1]
        for c in range(tm // CHUNK):
            rsl = slice(c * CHUNK, (c + 1) * CHUNK)
            mixed = _dot(ws, v_ref[rsl, gsl].astype(BF16)) + bias
            gated_ref[rsl, gsl] = (u_ref[rsl, gsl] * mixed).astype(BF16)
    m = _dot(gated_ref[...], wout_ref[...])
    o_ref[...] = x_ref[...] + _rms(m, gpost_ref[...])


def _sgu(x, norm_g, w_in, ln_g, ln_b, w_spatial, b_spatial_t, w_out, layer, j):
    seq, d = x.shape
    sgu_dim = w_out.shape[1]
    tm = ROW_TILE
    assert seq % tm == 0 and tm % CHUNK == 0 and sgu_dim % SGU_COL_CHUNK == 0
    return pl.pallas_call(
        _sgu_kernel,
        grid=(seq // tm,),
        in_specs=[
            pl.BlockSpec((tm, d), lambda i: (i, 0)),
            _resident((None, None, 1, d), lambda i: (layer, 2, 0, 0)),
            _resident((None, None, 1, d), lambda i: (layer, 3, 0, 0)),
            _resident((None, d, 2 * sgu_dim), lambda i: (j, 0, 0)),
            _resident((None, 1, sgu_dim), lambda i: (j, 0, 0)),
            _resident((None, 1, sgu_dim), lambda i: (j, 0, 0)),
            _resident((None, SGU_GROUPS, CHUNK, CHUNK), lambda i: (j, 0, 0, 0)),
            _resident((None, CHUNK, SGU_GROUPS), lambda i: (j, 0, 0)),
            _resident((None, sgu_dim, d), lambda i: (j, 0, 0)),
        ],
        out_specs=pl.BlockSpec((tm, d), lambda i: (i, 0)),
        out_shape=jax.ShapeDtypeStruct((seq, d), F32),
        scratch_shapes=[
            pltpu.VMEM((tm, d), BF16),
            pltpu.VMEM((tm, sgu_dim), F32),
            pltpu.VMEM((tm, sgu_dim), F32),
            pltpu.VMEM((tm, sgu_dim), BF16),
        ],
        compiler_params=pltpu.CompilerParams(
            dimension_semantics=("arbitrary",), vmem_limit_bytes=VMEM_LIMIT_BYTES),
        name=f"sgu_l{layer}",
    )(x, norm_g, norm_g, w_in, ln_g, ln_b, w_spatial, b_spatial_t, w_out)


def _conv_kernel(x_ref, gpre_ref, gpost_ref, pw1_ref, wdw_ref, bdw_ref, lng_ref, lnb_ref, pw2_ref,
                 o_ref, h_ref, yext_ref, conv_ref):
    tm, d = x_ref.shape
    cdim = pw2_ref.shape[0]

    @pl.when(pl.program_id(0) == 0)
    def _():
        yext_ref[0:CONV_HALO, :] = jnp.zeros((CONV_HALO, cdim), F32)

    h_ref[...] = _rms(x_ref[...], gpre_ref[...]).astype(BF16)
    h = h_ref[...]
    a = _dot(h, pw1_ref[:, 0:cdim])
    gate = _dot(h, pw1_ref[:, cdim:2 * cdim])
    yext_ref[CONV_HALO:CONV_HALO + tm, :] = a * jax.nn.sigmoid(gate)

    first = CONV_HALO - (CONV_WIDTH - 1)

    def block(b, carry):
        base = pl.multiple_of(b * CONV_ROW_BLOCK, CONV_ROW_BLOCK)
        acc = jnp.broadcast_to(bdw_ref[...], (CONV_ROW_BLOCK, cdim))
        window = yext_ref[pl.ds(base, CONV_ROW_BLOCK + CONV_HALO), :]
        for k in range(CONV_WIDTH):
            acc = acc + wdw_ref[k:k + 1, :] * window[first + k:first + k + CONV_ROW_BLOCK, :]
        conv_ref[pl.ds(base, CONV_ROW_BLOCK), :] = acc
        return carry

    lax.fori_loop(0, tm // CONV_ROW_BLOCK, block, 0)
    yext_ref[0:CONV_HALO, :] = yext_ref[tm:tm + CONV_HALO, :]

    z = jax.nn.silu(_layer_norm(conv_ref[...], lng_ref[...], lnb_ref[...])).astype(BF16)
    m = _dot(z, pw2_ref[...])
    o_ref[...] = x_ref[...] + _rms(m, gpost_ref[...])


def _conv(x, norm_g, w_pw1, w_dw, b_dw, ln_g, ln_b, w_pw2, layer, j):
    seq, d = x.shape
    cdim = w_pw2.shape[1]
    tm = ROW_TILE
    assert seq % tm == 0 and tm % CONV_ROW_BLOCK == 0 and CONV_HALO >= CONV_WIDTH - 1
    return pl.pallas_call(
        _conv_kernel,
        grid=(seq // tm,),
        in_specs=[
            pl.BlockSpec((tm, d), lambda i: (i, 0)),
            _resident((None, None, 1, d), lambda i: (layer, 2, 0, 0)),
            _resident((None, None, 1, d), lambda i: (layer, 3, 0, 0)),
            _resident((None, d, 2 * cdim), lambda i: (j, 0, 0)),
            _resident((None, CONV_WIDTH, cdim), lambda i: (j, 0, 0)),
            _resident((None, 1, cdim), lambda i: (j, 0, 0)),
            _resident((None, 1, cdim), lambda i: (j, 0, 0)),
            _resident((None, 1, cdim), lambda i: (j, 0, 0)),
            _resident((None, cdim, d), lambda i: (j, 0, 0)),
        ],
        out_specs=pl.BlockSpec((tm, d), lambda i: (i, 0)),
        out_shape=jax.ShapeDtypeStruct((seq, d), F32),
        scratch_shapes=[
            pltpu.VMEM((tm, d), BF16),
            pltpu.VMEM((tm + CONV_HALO, cdim), F32),
            pltpu.VMEM((tm, cdim), F32),
        ],
        compiler_params=pltpu.CompilerParams(
            dimension_semantics=("arbitrary",), vmem_limit_bytes=VMEM_LIMIT_BYTES),
        name=f"conv_l{layer}",
    )(x, norm_g, norm_g, w_pw1, w_dw, b_dw, ln_g, ln_b, w_pw2)


def kernel(x, norm_g, ff_w_gate, ff_w_up, ff_w_down, sgu_w_in, sgu_ln_g, sgu_ln_b, sgu_w_spatial,
           sgu_b_spatial, sgu_w_out, conv_w_pw1, conv_w_dw, conv_b_dw, conv_ln_g, conv_ln_b,
           conv_w_pw2):
    batch, seq, d = x.shape
    depth = norm_g.shape[0]
    wg, wu, wd = (w.astype(BF16) for w in (ff_w_gate, ff_w_up, ff_w_down))
    w_in, w_out = sgu_w_in.astype(BF16), sgu_w_out.astype(BF16)
    w_pw1, w_pw2 = conv_w_pw1.astype(BF16), conv_w_pw2.astype(BF16)
    norm_g4 = norm_g[:, :, None, :]
    sgu_ln_g3, sgu_ln_b3 = sgu_ln_g[:, None, :], sgu_ln_b[:, None, :]
    b_spatial_t = jnp.swapaxes(sgu_b_spatial, 1, 2)
    conv_b_dw3, conv_ln_g3, conv_ln_b3 = (a[:, None, :] for a in (conv_b_dw, conv_ln_g, conv_ln_b))

    outs = []
    for b in range(batch):
        xb = x[b]
        for i in range(depth):
            j = i // 2
            xb = _ffn(xb, norm_g4, wg, wu, wd, i, 0)
            if i % 2 == 0:
                xb = _sgu(xb, norm_g4, w_in, sgu_ln_g3, sgu_ln_b3, sgu_w_spatial, b_spatial_t,
                          w_out, i, j)
            else:
                xb = _conv(xb, norm_g4, w_pw1, conv_w_dw, conv_b_dw3, conv_ln_g3, conv_ln_b3,
                           w_pw2, i, j)
            xb = _ffn(xb, norm_g4, wg, wu, wd, i, 1)
        outs.append(xb)
    return jnp.stack(outs)
```

```python
import functools

import jax
import jax.numpy as jnp
from jax import lax
from jax.experimental import pallas as pl
from jax.experimental.pallas import tpu as pltpu

EPS = 1e-6
FFN_RESIDUAL_SCALE = 0.5
SGU_GROUPS = 8
CHUNK = 128
CONV_WIDTH = 31
CONV_HALO = 32

ROW_TILE = 512
FFN_ROW_TILE = 1024
ROW_SUB = 512
FF_COL_CHUNK = 256
SGU_COL_CHUNK = 256
SGU_ROW_SUB = 512
CONV_ROW_BLOCK = 128
LANES = 128
VMEM_LIMIT_BYTES = 56 * 1024 * 1024

F32 = jnp.float32
BF16 = jnp.bfloat16


def _rms(x, g):
    return x * lax.rsqrt(jnp.mean(x * x, axis=-1, keepdims=True) + EPS) * g


def _layer_norm(x, g, b):
    mu = jnp.mean(x, axis=-1, keepdims=True)
    xc = x - mu
    var = jnp.mean(xc * xc, axis=-1, keepdims=True)
    return xc * lax.rsqrt(var + EPS) * g + b


def _dot(a, b):
    return jnp.dot(a, b, preferred_element_type=F32)


def _ffn_kernel(x_ref, gpre_ref, gpost_ref, wg_ref, wu_ref, wd_ref, o_ref, h_ref, act_ref):
    d_ff = wg_ref.shape[1]
    for s in range(x_ref.shape[0] // ROW_SUB):
        rows = slice(s * ROW_SUB, (s + 1) * ROW_SUB)
        h_ref[rows, :] = _rms(x_ref[rows, :], gpre_ref[...]).astype(BF16)
        for j in range(d_ff // FF_COL_CHUNK):
            sl = slice(j * FF_COL_CHUNK, (j + 1) * FF_COL_CHUNK)
            h = h_ref[rows, :]
            gate = _dot(h, wg_ref[:, sl])
            up = _dot(h, wu_ref[:, sl])
            act_ref[rows, sl] = (jax.nn.silu(gate) * up).astype(BF16)
        y = _dot(act_ref[rows, :], wd_ref[...])
        o_ref[rows, :] = x_ref[rows, :] + FFN_RESIDUAL_SCALE * _rms(y, gpost_ref[...])


def _resident(shape, index_map):
    return pl.BlockSpec(shape, index_map, pipeline_mode=pl.Buffered(1))


def _ffn(x, norm_g, wg, wu, wd, layer, half):
    seq, d = x.shape
    d_ff = wg.shape[-1]
    tm = FFN_ROW_TILE
    assert seq % tm == 0 and tm % ROW_SUB == 0 and d_ff % FF_COL_CHUNK == 0
    g_pre, g_post = (0, 1) if half == 0 else (4, 5)
    return pl.pallas_call(
        _ffn_kernel,
        grid=(seq // tm,),
        in_specs=[
            pl.BlockSpec((tm, d), lambda i: (i, 0)),
            _resident((None, None, 1, d), lambda i: (layer, g_pre, 0, 0)),
            _resident((None, None, 1, d), lambda i: (layer, g_post, 0, 0)),
            _resident((None, None, d, d_ff), lambda i: (layer, half, 0, 0)),
            _resident((None, None, d, d_ff), lambda i: (layer, half, 0, 0)),
            _resident((None, None, d_ff, d), lambda i: (layer, half, 0, 0)),
        ],
        out_specs=pl.BlockSpec((tm, d), lambda i: (i, 0)),
        out_shape=jax.ShapeDtypeStruct((seq, d), F32),
        scratch_shapes=[pltpu.VMEM((tm, d), BF16), pltpu.VMEM((tm, d_ff), BF16)],
        compiler_params=pltpu.CompilerParams(
            dimension_semantics=("arbitrary",), vmem_limit_bytes=VMEM_LIMIT_BYTES),
        name=f"ffn_l{layer}_h{half}",
    )(x, norm_g, norm_g, wg, wu, wd)


def _sgu_kernel(x_ref, gpre_ref, gpost_ref, win_ref, lng_ref, lnb_ref, ws_ref, bs_ref, wout_ref,
                o_ref, h_ref, u_ref, v_ref, gated_ref):
    tm = x_ref.shape[0]
    sgu_dim = u_ref.shape[1]
    group_dim = sgu_dim // SGU_GROUPS
    row = lax.broadcasted_iota(jnp.int32, (CHUNK, CHUNK), 0)
    col = lax.broadcasted_iota(jnp.int32, (CHUNK, CHUNK), 1)
    causal = row >= col
    for s in range(tm // SGU_ROW_SUB):
        rows = slice(s * SGU_ROW_SUB, (s + 1) * SGU_ROW_SUB)
        h_ref[rows, :] = _rms(x_ref[rows, :], gpre_ref[...]).astype(BF16)
        for j in range(sgu_dim // SGU_COL_CHUNK):
            sl = slice(j * SGU_COL_CHUNK, (j + 1) * SGU_COL_CHUNK)
            slv = slice(sgu_dim + j * SGU_COL_CHUNK, sgu_dim + (j + 1) * SGU_COL_CHUNK)
            h = h_ref[rows, :]
            u_ref[rows, sl] = jax.nn.gelu(_dot(h, win_ref[:, sl]))
            v_ref[rows, sl] = jax.nn.gelu(_dot(h, win_ref[:, slv]))
        v_ref[rows, :] = _layer_norm(v_ref[rows, :], lng_ref[...], lnb_ref[...])
        for g in range(SGU_GROUPS):
            gsl = slice(g * group_dim, (g + 1) * group_dim)
            ws = jnp.where(causal, ws_ref[g], 0.0).astype(BF16)
            bias = bs_ref[:, g:g + 1]
            for c in range(SGU_ROW_SUB // CHUNK):
                rsl = slice(s * SGU_ROW_SUB + c * CHUNK, s * SGU_ROW_SUB + (c + 1) * CHUNK)
                mixed = _dot(ws, v_ref[rsl, gsl].astype(BF16)) + bias
                gated_ref[rsl, gsl] = (u_ref[rsl, gsl] * mixed).astype(BF16)
        m = _dot(gated_ref[rows, :], wout_ref[...])
        o_ref[rows, :] = x_ref[rows, :] + _rms(m, gpost_ref[...])


def _sgu(x, norm_g, w_in, ln_g, ln_b, w_spatial, b_spatial_t, w_out, layer, j):
    seq, d = x.shape
    sgu_dim = w_out.shape[1]
    tm = ROW_TILE
    assert seq % tm == 0 and tm % CHUNK == 0 and sgu_dim % SGU_COL_CHUNK == 0
    return pl.pallas_call(
        _sgu_kernel,
        grid=(seq // tm,),
        in_specs=[
            pl.BlockSpec((tm, d), lambda i: (i, 0)),
            _resident((None, None, 1, d), lambda i: (layer, 2, 0, 0)),
            _resident((None, None, 1, d), lambda i: (layer, 3, 0, 0)),
            _resident((None, d, 2 * sgu_dim), lambda i: (j, 0, 0)),
            _resident((None, 1, sgu_dim), lambda i: (j, 0, 0)),
            _resident((None, 1, sgu_dim), lambda i: (j, 0, 0)),
            _resident((None, SGU_GROUPS, CHUNK, CHUNK), lambda i: (j, 0, 0, 0)),
            _resident((None, CHUNK, SGU_GROUPS), lambda i: (j, 0, 0)),
            _resident((None, sgu_dim, d), lambda i: (j, 0, 0)),
        ],
        out_specs=pl.BlockSpec((tm, d), lambda i: (i, 0)),
        out_shape=jax.ShapeDtypeStruct((seq, d), F32),
        scratch_shapes=[
            pltpu.VMEM((tm, d), BF16),
            pltpu.VMEM((tm, sgu_dim), F32),
            pltpu.VMEM((tm, sgu_dim), F32),
            pltpu.VMEM((tm, sgu_dim), BF16),
        ],
        compiler_params=pltpu.CompilerParams(
            dimension_semantics=("arbitrary",), vmem_limit_bytes=VMEM_LIMIT_BYTES),
        name=f"sgu_l{layer}",
    )(x, norm_g, norm_g, w_in, ln_g, ln_b, w_spatial, b_spatial_t, w_out)


def _conv_kernel(x_ref, gpre_ref, gpost_ref, pw1_ref, wdw_ref, bdw_ref, lng_ref, lnb_ref, pw2_ref,
                 o_ref, h_ref, yext_ref, conv_ref):
    tm, d = x_ref.shape
    cdim = pw2_ref.shape[0]
    ncol = cdim // LANES

    @pl.when(pl.program_id(0) == 0)
    def _():
        yext_ref[:, 0:CONV_HALO, :] = jnp.zeros((ncol, CONV_HALO, LANES), F32)

    h_ref[...] = _rms(x_ref[...], gpre_ref[...]).astype(BF16)
    h = h_ref[...]
    a = _dot(h, pw1_ref[:, 0:cdim])
    gate = _dot(h, pw1_ref[:, cdim:2 * cdim])
    y = a * jax.nn.sigmoid(gate)
    for c in range(ncol):
        yext_ref[c, CONV_HALO:CONV_HALO + tm, :] = y[:, c * LANES:(c + 1) * LANES]

    first = CONV_HALO - (CONV_WIDTH - 1)
    half = CONV_ROW_BLOCK // 2

    for c in range(ncol):
        lanes = slice(c * LANES, (c + 1) * LANES)

        def block(b, carry, c=c, lanes=lanes):
            base = pl.multiple_of(b * CONV_ROW_BLOCK, CONV_ROW_BLOCK)
            bias = jnp.broadcast_to(bdw_ref[:, lanes], (half, LANES))
            even, odd = bias, bias
            tap = yext_ref[c, pl.ds(base + first, half, stride=2), :]
            for k in range(CONV_WIDTH):
                w = wdw_ref[k:k + 1, lanes]
                even = even + w * tap
                tap = yext_ref[c, pl.ds(base + first + k + 1, half, stride=2), :]
                odd = odd + w * tap
            conv_ref[c, pl.ds(base, half, stride=2), :] = even
            conv_ref[c, pl.ds(base + 1, half, stride=2), :] = odd
            return carry

        lax.fori_loop(0, tm // CONV_ROW_BLOCK, block, 0)
    yext_ref[:, 0:CONV_HALO, :] = yext_ref[:, tm:tm + CONV_HALO, :]

    conv = jnp.concatenate([conv_ref[c] for c in range(ncol)], axis=-1)
    z = jax.nn.silu(_layer_norm(conv, lng_ref[...], lnb_ref[...])).astype(BF16)
    m = _dot(z, pw2_ref[...])
    o_ref[...] = x_ref[...] + _rms(m, gpost_ref[...])


def _conv(x, norm_g, w_pw1, w_dw, b_dw, ln_g, ln_b, w_pw2, layer, j):
    seq, d = x.shape
    cdim = w_pw2.shape[1]
    tm = ROW_TILE
    assert seq % tm == 0 and tm % CONV_ROW_BLOCK == 0 and CONV_HALO >= CONV_WIDTH - 1
    return pl.pallas_call(
        _conv_kernel,
        grid=(seq // tm,),
        in_specs=[
            pl.BlockSpec((tm, d), lambda i: (i, 0)),
            _resident((None, None, 1, d), lambda i: (layer, 2, 0, 0)),
            _resident((None, None, 1, d), lambda i: (layer, 3, 0, 0)),
            _resident((None, d, 2 * cdim), lambda i: (j, 0, 0)),
            _resident((None, CONV_WIDTH, cdim), lambda i: (j, 0, 0)),
            _resident((None, 1, cdim), lambda i: (j, 0, 0)),
            _resident((None, 1, cdim), lambda i: (j, 0, 0)),
            _resident((None, 1, cdim), lambda i: (j, 0, 0)),
            _resident((None, cdim, d), lambda i: (j, 0, 0)),
        ],
        out_specs=pl.BlockSpec((tm, d), lambda i: (i, 0)),
        out_shape=jax.ShapeDtypeStruct((seq, d), F32),
        scratch_shapes=[
            pltpu.VMEM((tm, d), BF16),
            pltpu.VMEM((cdim // LANES, tm + CONV_HALO, LANES), F32),
            pltpu.VMEM((cdim // LANES, tm, LANES), F32),
        ],
        compiler_params=pltpu.CompilerParams(
            dimension_semantics=("arbitrary",), vmem_limit_bytes=VMEM_LIMIT_BYTES),
        name=f"conv_l{layer}",
    )(x, norm_g, norm_g, w_pw1, w_dw, b_dw, ln_g, ln_b, w_pw2)


def kernel(x, norm_g, ff_w_gate, ff_w_up, ff_w_down, sgu_w_in, sgu_ln_g, sgu_ln_b, sgu_w_spatial,
           sgu_b_spatial, sgu_w_out, conv_w_pw1, conv_w_dw, conv_b_dw, conv_ln_g, conv_ln_b,
           conv_w_pw2):
    batch, seq, d = x.shape
    depth = norm_g.shape[0]
    wg, wu, wd = (w.astype(BF16) for w in (ff_w_gate, ff_w_up, ff_w_down))
    w_in, w_out = sgu_w_in.astype(BF16), sgu_w_out.astype(BF16)
    w_pw1, w_pw2 = conv_w_pw1.astype(BF16), conv_w_pw2.astype(BF16)
    norm_g4 = norm_g[:, :, None, :]
    sgu_ln_g3, sgu_ln_b3 = sgu_ln_g[:, None, :], sgu_ln_b[:, None, :]
    b_spatial_t = jnp.swapaxes(sgu_b_spatial, 1, 2)
    conv_b_dw3, conv_ln_g3, conv_ln_b3 = (a[:, None, :] for a in (conv_b_dw, conv_ln_g, conv_ln_b))

    outs = []
    for b in range(batch):
        xb = x[b]
        for i in range(depth):
            j = i // 2
            xb = _ffn(xb, norm_g4, wg, wu, wd, i, 0)
            if i % 2 == 0:
                xb = _sgu(xb, norm_g4, w_in, sgu_ln_g3, sgu_ln_b3, sgu_w_spatial, b_spatial_t,
                          w_out, i, j)
            else:
                xb = _conv(xb, norm_g4, w_pw1, conv_w_dw, conv_b_dw3, conv_ln_g3, conv_ln_b3,
                           w_pw2, i, j)
            xb = _ffn(xb, norm_g4, wg, wu, wd, i, 1)
        outs.append(xb)
    return jnp.stack(outs)
```

```python
import functools

import jax
import jax.numpy as jnp
from jax import lax
from jax.experimental import pallas as pl
from jax.experimental.pallas import tpu as pltpu

EPS = 1e-6
FFN_RESIDUAL_SCALE = 0.5
SGU_GROUPS = 8
CHUNK = 128
CONV_WIDTH = 31
CONV_HALO = 32

ROW_TILE = 512
FFN_ROW_TILE = 1024
ROW_SUB = 512
FF_COL_CHUNK = 256
SGU_COL_CHUNK = 256
SGU_ROW_SUB = 512
CONV_ROW_BLOCK = 128
CONV_COL_CHUNK = 256
LANES = 128
BF16_SUBLANES = 16
VMEM_LIMIT_BYTES = 60 * 1024 * 1024

F32 = jnp.float32
BF16 = jnp.bfloat16


def _rms(x, g):
    return x * lax.rsqrt(jnp.mean(x * x, axis=-1, keepdims=True) + EPS) * g


def _layer_norm(x, g, b):
    mu = jnp.mean(x, axis=-1, keepdims=True)
    xc = x - mu
    var = jnp.mean(xc * xc, axis=-1, keepdims=True)
    return xc * lax.rsqrt(var + EPS) * g + b


def _dot(a, b):
    return jnp.dot(a, b, preferred_element_type=F32)


def _ffn_kernel(n_cast, x_ref, gpre_ref, gpost_ref, wg_ref, wu_ref, wd_ref, *refs):
    cast_in, o_ref, cast_out = refs[:n_cast], refs[n_cast], refs[n_cast + 1:2 * n_cast + 1]
    h_ref, act_ref = refs[2 * n_cast + 1:]
    d_ff = wg_ref.shape[1]
    for s in range(x_ref.shape[0] // ROW_SUB):
        rows = slice(s * ROW_SUB, (s + 1) * ROW_SUB)
        h_ref[rows, :] = _rms(x_ref[rows, :], gpre_ref[...]).astype(BF16)
        for j in range(d_ff // FF_COL_CHUNK):
            sl = slice(j * FF_COL_CHUNK, (j + 1) * FF_COL_CHUNK)
            h = h_ref[rows, :]
            gate = _dot(h, wg_ref[:, sl])
            up = _dot(h, wu_ref[:, sl])
            act_ref[rows, sl] = (jax.nn.silu(gate) * up).astype(BF16)
        y = _dot(act_ref[rows, :], wd_ref[...])
        o_ref[rows, :] = x_ref[rows, :] + FFN_RESIDUAL_SCALE * _rms(y, gpost_ref[...])
    for src_ref, dst_ref in zip(cast_in, cast_out):
        dst_ref[...] = src_ref[...].astype(BF16)


def _resident(shape, index_map):
    return pl.BlockSpec(shape, index_map, pipeline_mode=pl.Buffered(1))


def _cast_index(lead, i):
    return (*lead, i, 0)


def _ffn(x, norm_g, wg, wu, wd, layer, half, cast_srcs):
    seq, d = x.shape
    d_ff = wg.shape[-1]
    tm = FFN_ROW_TILE
    n_steps = seq // tm
    assert seq % tm == 0 and tm % ROW_SUB == 0 and d_ff % FF_COL_CHUNK == 0
    g_pre, g_post = (0, 1) if half == 0 else (4, 5)
    cast_in_specs, cast_out_specs, cast_out_shapes = [], [], []
    for w, lead in cast_srcs:
        rows, cols = w.shape[-2:]
        assert rows % (n_steps * BF16_SUBLANES) == 0
        rb = rows // n_steps
        cast_in_specs.append(pl.BlockSpec(
            (None,) * len(lead) + (rb, cols), functools.partial(_cast_index, lead)))
        cast_out_specs.append(pl.BlockSpec((rb, cols), lambda i: (i, 0)))
        cast_out_shapes.append(jax.ShapeDtypeStruct((rows, cols), BF16))
    outs = pl.pallas_call(
        functools.partial(_ffn_kernel, len(cast_srcs)),
        grid=(n_steps,),
        in_specs=[
            pl.BlockSpec((tm, d), lambda i: (i, 0)),
            _resident((None, None, 1, d), lambda i: (layer, g_pre, 0, 0)),
            _resident((None, None, 1, d), lambda i: (layer, g_post, 0, 0)),
            _resident((d, d_ff), lambda i: (0, 0)),
            _resident((d, d_ff), lambda i: (0, 0)),
            _resident((d_ff, d), lambda i: (0, 0)),
        ] + cast_in_specs,
        out_specs=[pl.BlockSpec((tm, d), lambda i: (i, 0))] + cast_out_specs,
        out_shape=[jax.ShapeDtypeStruct((seq, d), F32)] + cast_out_shapes,
        scratch_shapes=[pltpu.VMEM((tm, d), BF16), pltpu.VMEM((tm, d_ff), BF16)],
        compiler_params=pltpu.CompilerParams(
            dimension_semantics=("arbitrary",), vmem_limit_bytes=VMEM_LIMIT_BYTES),
        name=f"ffn_l{layer}_h{half}",
    )(x, norm_g, norm_g, wg, wu, wd, *(w for w, _ in cast_srcs))
    return outs[0], outs[1:]


def _sgu_kernel(x_ref, gpre_ref, gpost_ref, win_ref, lng_ref, lnb_ref, ws_ref, bs_ref, wout_ref,
                o_ref, h_ref, u_ref, v_ref, gated_ref):
    tm = x_ref.shape[0]
    sgu_dim = u_ref.shape[1]
    group_dim = sgu_dim // SGU_GROUPS
    row = lax.broadcasted_iota(jnp.int32, (CHUNK, CHUNK), 0)
    col = lax.broadcasted_iota(jnp.int32, (CHUNK, CHUNK), 1)
    causal = row >= col
    for s in range(tm // SGU_ROW_SUB):
        rows = slice(s * SGU_ROW_SUB, (s + 1) * SGU_ROW_SUB)
        h_ref[rows, :] = _rms(x_ref[rows, :], gpre_ref[...]).astype(BF16)
        for j in range(sgu_dim // SGU_COL_CHUNK):
            sl = slice(j * SGU_COL_CHUNK, (j + 1) * SGU_COL_CHUNK)
            slv = slice(sgu_dim + j * SGU_COL_CHUNK, sgu_dim + (j + 1) * SGU_COL_CHUNK)
            h = h_ref[rows, :]
            v_ref[rows, sl] = jax.nn.gelu(_dot(h, win_ref[:, slv]))
        for j in range(sgu_dim // SGU_COL_CHUNK):
            sl = slice(j * SGU_COL_CHUNK, (j + 1) * SGU_COL_CHUNK)
            h = h_ref[rows, :]
            u_ref[rows, sl] = jax.nn.gelu(_dot(h, win_ref[:, sl]))
        v_ref[rows, :] = _layer_norm(v_ref[rows, :], lng_ref[...], lnb_ref[...])
        for g in range(SGU_GROUPS):
            gsl = slice(g * group_dim, (g + 1) * group_dim)
            ws = jnp.where(causal, ws_ref[g], 0.0).astype(BF16)
            bias = bs_ref[:, g:g + 1]
            for c in range(SGU_ROW_SUB // CHUNK):
                rsl = slice(s * SGU_ROW_SUB + c * CHUNK, s * SGU_ROW_SUB + (c + 1) * CHUNK)
                mixed = _dot(ws, v_ref[rsl, gsl].astype(BF16)) + bias
                gated_ref[rsl, gsl] = (u_ref[rsl, gsl] * mixed).astype(BF16)
        m = _dot(gated_ref[rows, :], wout_ref[...])
        o_ref[rows, :] = x_ref[rows, :] + _rms(m, gpost_ref[...])


def _sgu(x, norm_g, w_in, ln_g, ln_b, w_spatial, b_spatial_t, w_out, layer, j):
    seq, d = x.shape
    sgu_dim = w_out.shape[0]
    tm = ROW_TILE
    assert seq % tm == 0 and tm % SGU_ROW_SUB == 0 and SGU_ROW_SUB % CHUNK == 0
    assert sgu_dim % SGU_COL_CHUNK == 0
    return pl.pallas_call(
        _sgu_kernel,
        grid=(seq // tm,),
        in_specs=[
            pl.BlockSpec((tm, d), lambda i: (i, 0)),
            _resident((None, None, 1, d), lambda i: (layer, 2, 0, 0)),
            _resident((None, None, 1, d), lambda i: (layer, 3, 0, 0)),
            _resident((d, 2 * sgu_dim), lambda i: (0, 0)),
            _resident((None, 1, sgu_dim), lambda i: (j, 0, 0)),
            _resident((None, 1, sgu_dim), lambda i: (j, 0, 0)),
            _resident((None, SGU_GROUPS, CHUNK, CHUNK), lambda i: (j, 0, 0, 0)),
            _resident((None, CHUNK, SGU_GROUPS), lambda i: (j, 0, 0)),
            _resident((sgu_dim, d), lambda i: (0, 0)),
        ],
        out_specs=pl.BlockSpec((tm, d), lambda i: (i, 0)),
        out_shape=jax.ShapeDtypeStruct((seq, d), F32),
        scratch_shapes=[
            pltpu.VMEM((tm, d), BF16),
            pltpu.VMEM((tm, sgu_dim), F32),
            pltpu.VMEM((tm, sgu_dim), F32),
            pltpu.VMEM((tm, sgu_dim), BF16),
        ],
        compiler_params=pltpu.CompilerParams(
            dimension_semantics=("arbitrary",), vmem_limit_bytes=VMEM_LIMIT_BYTES),
        name=f"sgu_l{layer}",
    )(x, norm_g, norm_g, w_in, ln_g, ln_b, w_spatial, b_spatial_t, w_out)


def _conv_kernel(x_ref, gpre_ref, gpost_ref, pw1_ref, wdw_ref, bdw_ref, lng_ref, lnb_ref, pw2_ref,
                 o_ref, h_ref, yext_ref, conv_ref):
    tm, d = x_ref.shape
    cdim = pw2_ref.shape[0]
    ncol = cdim // LANES

    @pl.when(pl.program_id(0) == 0)
    def _():
        yext_ref[:, 0:CONV_HALO, :] = jnp.zeros((ncol, CONV_HALO, LANES), F32)

    h_ref[...] = _rms(x_ref[...], gpre_ref[...]).astype(BF16)
    first = CONV_HALO - (CONV_WIDTH - 1)
    half = CONV_ROW_BLOCK // 2
    for cc in range(cdim // CONV_COL_CHUNK):
        h = h_ref[...]
        a = _dot(h, pw1_ref[:, cc * CONV_COL_CHUNK:(cc + 1) * CONV_COL_CHUNK])
        gate = _dot(h, pw1_ref[:, cdim + cc * CONV_COL_CHUNK:cdim + (cc + 1) * CONV_COL_CHUNK])
        y = a * jax.nn.sigmoid(gate)
        for cl in range(CONV_COL_CHUNK // LANES):
            c = cc * (CONV_COL_CHUNK // LANES) + cl
            lanes = slice(c * LANES, (c + 1) * LANES)
            yext_ref[c, CONV_HALO:CONV_HALO + tm, :] = y[:, cl * LANES:(cl + 1) * LANES]
            for b in range(tm // CONV_ROW_BLOCK):
                base = b * CONV_ROW_BLOCK
                bias = jnp.broadcast_to(bdw_ref[:, lanes], (half, LANES))
                even, odd = bias, bias
                tap = yext_ref[c, pl.ds(base + first, half, stride=2), :]
                for k in range(CONV_WIDTH):
                    w = wdw_ref[k:k + 1, lanes]
                    even = even + w * tap
                    tap = yext_ref[c, pl.ds(base + first + k + 1, half, stride=2), :]
                    odd = odd + w * tap
                conv_ref[c, pl.ds(base, half, stride=2), :] = even
                conv_ref[c, pl.ds(base + 1, half, stride=2), :] = odd
            yext_ref[c, 0:CONV_HALO, :] = yext_ref[c, tm:tm + CONV_HALO, :]

    conv = jnp.concatenate([conv_ref[c] for c in range(ncol)], axis=-1)
    z = jax.nn.silu(_layer_norm(conv, lng_ref[...], lnb_ref[...])).astype(BF16)
    m = _dot(z, pw2_ref[...])
    o_ref[...] = x_ref[...] + _rms(m, gpost_ref[...])


def _conv(x, norm_g, w_pw1, w_dw, b_dw, ln_g, ln_b, w_pw2, layer, j):
    seq, d = x.shape
    cdim = w_pw2.shape[0]
    tm = ROW_TILE
    assert seq % tm == 0 and tm % CONV_ROW_BLOCK == 0 and CONV_HALO >= CONV_WIDTH - 1
    assert cdim % CONV_COL_CHUNK == 0 and CONV_COL_CHUNK % LANES == 0
    return pl.pallas_call(
        _conv_kernel,
        grid=(seq // tm,),
        in_specs=[
            pl.BlockSpec((tm, d), lambda i: (i, 0)),
            _resident((None, None, 1, d), lambda i: (layer, 2, 0, 0)),
            _resident((None, None, 1, d), lambda i: (layer, 3, 0, 0)),
            _resident((d, 2 * cdim), lambda i: (0, 0)),
            _resident((None, CONV_WIDTH, cdim), lambda i: (j, 0, 0)),
            _resident((None, 1, cdim), lambda i: (j, 0, 0)),
            _resident((None, 1, cdim), lambda i: (j, 0, 0)),
            _resident((None, 1, cdim), lambda i: (j, 0, 0)),
            _resident((cdim, d), lambda i: (0, 0)),
        ],
        out_specs=pl.BlockSpec((tm, d), lambda i: (i, 0)),
        out_shape=jax.ShapeDtypeStruct((seq, d), F32),
        scratch_shapes=[
            pltpu.VMEM((tm, d), BF16),
            pltpu.VMEM((cdim // LANES, tm + CONV_HALO, LANES), F32),
            pltpu.VMEM((cdim // LANES, tm, LANES), F32),
        ],
        compiler_params=pltpu.CompilerParams(
            dimension_semantics=("arbitrary",), vmem_limit_bytes=VMEM_LIMIT_BYTES),
        name=f"conv_l{layer}",
    )(x, norm_g, norm_g, w_pw1, w_dw, b_dw, ln_g, ln_b, w_pw2)


def kernel(x, norm_g, ff_w_gate, ff_w_up, ff_w_down, sgu_w_in, sgu_ln_g, sgu_ln_b, sgu_w_spatial,
           sgu_b_spatial, sgu_w_out, conv_w_pw1, conv_w_dw, conv_b_dw, conv_ln_g, conv_ln_b,
           conv_w_pw2):
    batch, seq, d = x.shape
    depth = norm_g.shape[0]
    norm_g4 = norm_g[:, :, None, :]
    sgu_ln_g3, sgu_ln_b3 = sgu_ln_g[:, None, :], sgu_ln_b[:, None, :]
    b_spatial_t = jnp.swapaxes(sgu_b_spatial, 1, 2)
    conv_b_dw3, conv_ln_g3, conv_ln_b3 = (a[:, None, :] for a in (conv_b_dw, conv_ln_g, conv_ln_b))

    def ffn_srcs(layer, half):
        return [(w, (layer, half)) for w in (ff_w_gate, ff_w_up, ff_w_down)]

    outs = []
    for b in range(batch):
        xb = x[b]
        ffn_w = [w[0, 0].astype(BF16) for w in (ff_w_gate, ff_w_up, ff_w_down)]
        for i in range(depth):
            j = i // 2
            if i % 2 == 0:
                mixer_srcs = [(sgu_w_in, (j,)), (sgu_w_out, (j,))]
            else:
                mixer_srcs = [(conv_w_pw1, (j,)), (conv_w_pw2, (j,))]
            xb, cast = _ffn(xb, norm_g4, *ffn_w, i, 0, mixer_srcs + ffn_srcs(i, 1))
            mixer_w, ffn_w = cast[:2], cast[2:]
            if i % 2 == 0:
                xb = _sgu(xb, norm_g4, mixer_w[0], sgu_ln_g3, sgu_ln_b3, sgu_w_spatial,
                          b_spatial_t, mixer_w[1], i, j)
            else:
                xb = _conv(xb, norm_g4, mixer_w[0], conv_w_dw, conv_b_dw3, conv_ln_g3, conv_ln_b3,
                           mixer_w[1], i, j)
            next_srcs = ffn_srcs(i + 1, 0) if i + 1 < depth else []
            xb, ffn_w = _ffn(xb, norm_g4, *ffn_w, i, 1, next_srcs)
        outs.append(xb)
    return jnp.stack(outs)
```

```python
import functools

import jax
import jax.numpy as jnp
from jax import lax
from jax.experimental import pallas as pl
from jax.experimental.pallas import tpu as pltpu

EPS = 1e-6
FFN_RESIDUAL_SCALE = 0.5
SGU_GROUPS = 8
CHUNK = 128
CONV_WIDTH = 31
CONV_HALO = 32

ROW_TILE = 1024
FFN_ROW_TILE = 1024
ROW_SUB = 512
FF_COL_CHUNK = 256
SGU_COL_CHUNK = 256
SGU_ROW_SUB = 512
SGU_TAIL_ROWS = 256
CONV_ROW_BLOCK = 128
CONV_COL_CHUNK = 256
LANES = 128
BF16_SUBLANES = 16
VMEM_LIMIT_BYTES = 60 * 1024 * 1024

F32 = jnp.float32
BF16 = jnp.bfloat16


def _rms(x, g):
    return x * lax.rsqrt(jnp.mean(x * x, axis=-1, keepdims=True) + EPS) * g


def _layer_norm(x, g, b):
    mu = jnp.mean(x, axis=-1, keepdims=True)
    xc = x - mu
    var = jnp.mean(xc * xc, axis=-1, keepdims=True)
    return xc * lax.rsqrt(var + EPS) * g + b


def _dot(a, b):
    return jnp.dot(a, b, preferred_element_type=F32)


def _ffn_kernel(n_cast, x_ref, gpre_ref, gpost_ref, wg_ref, wu_ref, wd_ref, *refs):
    cast_in, o_ref, cast_out = refs[:n_cast], refs[n_cast], refs[n_cast + 1:2 * n_cast + 1]
    h_ref, act_ref = refs[2 * n_cast + 1:]
    d_ff = wg_ref.shape[1]
    for s in range(x_ref.shape[0] // ROW_SUB):
        rows = slice(s * ROW_SUB, (s + 1) * ROW_SUB)
        h_ref[rows, :] = _rms(x_ref[rows, :], gpre_ref[...]).astype(BF16)
        for j in range(d_ff // FF_COL_CHUNK):
            sl = slice(j * FF_COL_CHUNK, (j + 1) * FF_COL_CHUNK)
            h = h_ref[rows, :]
            gate = _dot(h, wg_ref[:, sl])
            up = _dot(h, wu_ref[:, sl])
            act_ref[rows, sl] = (jax.nn.silu(gate) * up).astype(BF16)
        y = _dot(act_ref[rows, :], wd_ref[...])
        o_ref[rows, :] = x_ref[rows, :] + FFN_RESIDUAL_SCALE * _rms(y, gpost_ref[...])
    for src_ref, dst_ref in zip(cast_in, cast_out):
        dst_ref[...] = src_ref[...].astype(BF16)


def _resident(shape, index_map):
    return pl.BlockSpec(shape, index_map, pipeline_mode=pl.Buffered(1))


def _cast_index(lead, i):
    return (*lead, i, 0)


def _ffn(x, norm_g, wg, wu, wd, layer, half, cast_srcs):
    seq, d = x.shape
    d_ff = wg.shape[-1]
    tm = FFN_ROW_TILE
    n_steps = seq // tm
    assert seq % tm == 0 and tm % ROW_SUB == 0 and d_ff % FF_COL_CHUNK == 0
    g_pre, g_post = (0, 1) if half == 0 else (4, 5)
    cast_in_specs, cast_out_specs, cast_out_shapes = [], [], []
    for w, lead in cast_srcs:
        rows, cols = w.shape[-2:]
        assert rows % (n_steps * BF16_SUBLANES) == 0
        rb = rows // n_steps
        cast_in_specs.append(pl.BlockSpec(
            (None,) * len(lead) + (rb, cols), functools.partial(_cast_index, lead)))
        cast_out_specs.append(pl.BlockSpec((rb, cols), lambda i: (i, 0)))
        cast_out_shapes.append(jax.ShapeDtypeStruct((rows, cols), BF16))
    outs = pl.pallas_call(
        functools.partial(_ffn_kernel, len(cast_srcs)),
        grid=(n_steps,),
        in_specs=[
            pl.BlockSpec((tm, d), lambda i: (i, 0)),
            _resident((None, None, 1, d), lambda i: (layer, g_pre, 0, 0)),
            _resident((None, None, 1, d), lambda i: (layer, g_post, 0, 0)),
            _resident((d, d_ff), lambda i: (0, 0)),
            _resident((d, d_ff), lambda i: (0, 0)),
            _resident((d_ff, d), lambda i: (0, 0)),
        ] + cast_in_specs,
        out_specs=[pl.BlockSpec((tm, d), lambda i: (i, 0))] + cast_out_specs,
        out_shape=[jax.ShapeDtypeStruct((seq, d), F32)] + cast_out_shapes,
        scratch_shapes=[pltpu.VMEM((tm, d), BF16), pltpu.VMEM((tm, d_ff), BF16)],
        compiler_params=pltpu.CompilerParams(
            dimension_semantics=("arbitrary",), vmem_limit_bytes=VMEM_LIMIT_BYTES),
        name=f"ffn_l{layer}_h{half}",
    )(x, norm_g, norm_g, wg, wu, wd, *(w for w, _ in cast_srcs))
    return outs[0], outs[1:]


def _sgu_kernel(x_ref, gpre_ref, gpost_ref, win_ref, lng_ref, lnb_ref, ws_ref, bs_ref, wout_ref,
                o_ref, h_ref, u_ref, v_ref, gated_ref):
    tm = x_ref.shape[0]
    sgu_dim = u_ref.shape[1]
    group_dim = sgu_dim // SGU_GROUPS
    row = lax.broadcasted_iota(jnp.int32, (CHUNK, CHUNK), 0)
    col = lax.broadcasted_iota(jnp.int32, (CHUNK, CHUNK), 1)
    causal = row >= col
    for s in range(tm // SGU_ROW_SUB):
        rows = slice(s * SGU_ROW_SUB, (s + 1) * SGU_ROW_SUB)
        h_ref[rows, :] = _rms(x_ref[rows, :], gpre_ref[...]).astype(BF16)
        for j in range(sgu_dim // SGU_COL_CHUNK):
            sl = slice(j * SGU_COL_CHUNK, (j + 1) * SGU_COL_CHUNK)
            slv = slice(sgu_dim + j * SGU_COL_CHUNK, sgu_dim + (j + 1) * SGU_COL_CHUNK)
            h = h_ref[rows, :]
            v_ref[rows, sl] = jax.nn.gelu(_dot(h, win_ref[:, slv]))
        for j in range(sgu_dim // SGU_COL_CHUNK):
            sl = slice(j * SGU_COL_CHUNK, (j + 1) * SGU_COL_CHUNK)
            h = h_ref[rows, :]
            u_ref[rows, sl] = jax.nn.gelu(_dot(h, win_ref[:, sl]))
        for t in range(SGU_ROW_SUB // SGU_TAIL_ROWS):
            r0 = s * SGU_ROW_SUB + t * SGU_TAIL_ROWS
            trows = slice(r0, r0 + SGU_TAIL_ROWS)
            v_ref[trows, :] = _layer_norm(v_ref[trows, :], lng_ref[...], lnb_ref[...])
            for g in range(SGU_GROUPS):
                gsl = slice(g * group_dim, (g + 1) * group_dim)
                ws = jnp.where(causal, ws_ref[g], 0.0).astype(BF16)
                bias = bs_ref[:, g:g + 1]
                for c in range(SGU_TAIL_ROWS // CHUNK):
                    rsl = slice(r0 + c * CHUNK, r0 + (c + 1) * CHUNK)
                    mixed = _dot(ws, v_ref[rsl, gsl].astype(BF16)) + bias
                    gated_ref[rsl, gsl] = (u_ref[rsl, gsl] * mixed).astype(BF16)
            m = _dot(gated_ref[trows, :], wout_ref[...])
            o_ref[trows, :] = x_ref[trows, :] + _rms(m, gpost_ref[...])


def _sgu(x, norm_g, w_in, ln_g, ln_b, w_spatial, b_spatial_t, w_out, layer, j):
    seq, d = x.shape
    sgu_dim = w_out.shape[0]
    tm = ROW_TILE
    assert seq % tm == 0 and tm % SGU_ROW_SUB == 0
    assert SGU_ROW_SUB % SGU_TAIL_ROWS == 0 and SGU_TAIL_ROWS % CHUNK == 0
    assert sgu_dim % SGU_COL_CHUNK == 0
    return pl.pallas_call(
        _sgu_kernel,
        grid=(seq // tm,),
        in_specs=[
            pl.BlockSpec((tm, d), lambda i: (i, 0)),
            _resident((None, None, 1, d), lambda i: (layer, 2, 0, 0)),
            _resident((None, None, 1, d), lambda i: (layer, 3, 0, 0)),
            _resident((d, 2 * sgu_dim), lambda i: (0, 0)),
            _resident((None, 1, sgu_dim), lambda i: (j, 0, 0)),
            _resident((None, 1, sgu_dim), lambda i: (j, 0, 0)),
            _resident((None, SGU_GROUPS, CHUNK, CHUNK), lambda i: (j, 0, 0, 0)),
            _resident((None, CHUNK, SGU_GROUPS), lambda i: (j, 0, 0)),
            _resident((sgu_dim, d), lambda i: (0, 0)),
        ],
        out_specs=pl.BlockSpec((tm, d), lambda i: (i, 0)),
        out_shape=jax.ShapeDtypeStruct((seq, d), F32),
        scratch_shapes=[
            pltpu.VMEM((tm, d), BF16),
            pltpu.VMEM((tm, sgu_dim), F32),
            pltpu.VMEM((tm, sgu_dim), F32),
            pltpu.VMEM((tm, sgu_dim), BF16),
        ],
        compiler_params=pltpu.CompilerParams(
            dimension_semantics=("arbitrary",), vmem_limit_bytes=VMEM_LIMIT_BYTES),
        name=f"sgu_l{layer}",
    )(x, norm_g, norm_g, w_in, ln_g, ln_b, w_spatial, b_spatial_t, w_out)


def _conv_kernel(x_ref, gpre_ref, gpost_ref, pw1_ref, wdw_ref, bdw_ref, lng_ref, lnb_ref, pw2_ref,
                 o_ref, h_ref, yext_ref, conv_ref):
    tm, d = x_ref.shape
    cdim = pw2_ref.shape[0]
    ncol = cdim // LANES

    @pl.when(pl.program_id(0) == 0)
    def _():
        yext_ref[:, 0:CONV_HALO, :] = jnp.zeros((ncol, CONV_HALO, LANES), F32)

    h_ref[...] = _rms(x_ref[...], gpre_ref[...]).astype(BF16)
    first = CONV_HALO - (CONV_WIDTH - 1)
    half = CONV_ROW_BLOCK // 2
    for cc in range(cdim // CONV_COL_CHUNK):
        h = h_ref[...]
        a = _dot(h, pw1_ref[:, cc * CONV_COL_CHUNK:(cc + 1) * CONV_COL_CHUNK])
        gate = _dot(h, pw1_ref[:, cdim + cc * CONV_COL_CHUNK:cdim + (cc + 1) * CONV_COL_CHUNK])
        y = a * jax.nn.sigmoid(gate)
        for cl in range(CONV_COL_CHUNK // LANES):
            c = cc * (CONV_COL_CHUNK // LANES) + cl
            lanes = slice(c * LANES, (c + 1) * LANES)
            yext_ref[c, CONV_HALO:CONV_HALO + tm, :] = y[:, cl * LANES:(cl + 1) * LANES]
            for b in range(tm // CONV_ROW_BLOCK):
                base = b * CONV_ROW_BLOCK
                bias = jnp.broadcast_to(bdw_ref[:, lanes], (half, LANES))
                even, odd = bias, bias
                tap = yext_ref[c, pl.ds(base + first, half, stride=2), :]
                for k in range(CONV_WIDTH):
                    w = wdw_ref[k:k + 1, lanes]
                    even = even + w * tap
                    tap = yext_ref[c, pl.ds(base + first + k + 1, half, stride=2), :]
                    odd = odd + w * tap
                conv_ref[c, pl.ds(base, half, stride=2), :] = even
                conv_ref[c, pl.ds(base + 1, half, stride=2), :] = odd
            yext_ref[c, 0:CONV_HALO, :] = yext_ref[c, tm:tm + CONV_HALO, :]

    conv = jnp.concatenate([conv_ref[c] for c in range(ncol)], axis=-1)
    z = jax.nn.silu(_layer_norm(conv, lng_ref[...], lnb_ref[...])).astype(BF16)
    m = _dot(z, pw2_ref[...])
    o_ref[...] = x_ref[...] + _rms(m, gpost_ref[...])


def _conv(x, norm_g, w_pw1, w_dw, b_dw, ln_g, ln_b, w_pw2, layer, j):
    seq, d = x.shape
    cdim = w_pw2.shape[0]
    tm = ROW_TILE
    assert seq % tm == 0 and tm % CONV_ROW_BLOCK == 0 and CONV_HALO >= CONV_WIDTH - 1
    assert cdim % CONV_COL_CHUNK == 0 and CONV_COL_CHUNK % LANES == 0
    return pl.pallas_call(
        _conv_kernel,
        grid=(seq // tm,),
        in_specs=[
            pl.BlockSpec((tm, d), lambda i: (i, 0)),
            _resident((None, None, 1, d), lambda i: (layer, 2, 0, 0)),
            _resident((None, None, 1, d), lambda i: (layer, 3, 0, 0)),
            _resident((d, 2 * cdim), lambda i: (0, 0)),
            _resident((None, CONV_WIDTH, cdim), lambda i: (j, 0, 0)),
            _resident((None, 1, cdim), lambda i: (j, 0, 0)),
            _resident((None, 1, cdim), lambda i: (j, 0, 0)),
            _resident((None, 1, cdim), lambda i: (j, 0, 0)),
            _resident((cdim, d), lambda i: (0, 0)),
        ],
        out_specs=pl.BlockSpec((tm, d), lambda i: (i, 0)),
        out_shape=jax.ShapeDtypeStruct((seq, d), F32),
        scratch_shapes=[
            pltpu.VMEM((tm, d), BF16),
            pltpu.VMEM((cdim // LANES, tm + CONV_HALO, LANES), F32),
            pltpu.VMEM((cdim // LANES, tm, LANES), F32),
        ],
        compiler_params=pltpu.CompilerParams(
            dimension_semantics=("arbitrary",), vmem_limit_bytes=VMEM_LIMIT_BYTES),
        name=f"conv_l{layer}",
    )(x, norm_g, norm_g, w_pw1, w_dw, b_dw, ln_g, ln_b, w_pw2)


def kernel(x, norm_g, ff_w_gate, ff_w_up, ff_w_down, sgu_w_in, sgu_ln_g, sgu_ln_b, sgu_w_spatial,
           sgu_b_spatial, sgu_w_out, conv_w_pw1, conv_w_dw, conv_b_dw, conv_ln_g, conv_ln_b,
           conv_w_pw2):
    batch, seq, d = x.shape
    depth = norm_g.shape[0]
    norm_g4 = norm_g[:, :, None, :]
    sgu_ln_g3, sgu_ln_b3 = sgu_ln_g[:, None, :], sgu_ln_b[:, None, :]
    b_spatial_t = jnp.swapaxes(sgu_b_spatial, 1, 2)
    conv_b_dw3, conv_ln_g3, conv_ln_b3 = (a[:, None, :] for a in (conv_b_dw, conv_ln_g, conv_ln_b))

    def ffn_srcs(layer, half):
        return [(w, (layer, half)) for w in (ff_w_gate, ff_w_up, ff_w_down)]

    outs = []
    for b in range(batch):
        xb = x[b]
        ffn_w = [w[0, 0].astype(BF16) for w in (ff_w_gate, ff_w_up, ff_w_down)]
        for i in range(depth):
            j = i // 2
            if i % 2 == 0:
                mixer_srcs = [(sgu_w_in, (j,)), (sgu_w_out, (j,))]
            else:
                mixer_srcs = [(conv_w_pw1, (j,)), (conv_w_pw2, (j,))]
            xb, cast = _ffn(xb, norm_g4, *ffn_w, i, 0, mixer_srcs + ffn_srcs(i, 1))
            mixer_w, ffn_w = cast[:2], cast[2:]
            if i % 2 == 0:
                xb = _sgu(xb, norm_g4, mixer_w[0], sgu_ln_g3, sgu_ln_b3, sgu_w_spatial,
                          b_spatial_t, mixer_w[1], i, j)
            else:
                xb = _conv(xb, norm_g4, mixer_w[0], conv_w_dw, conv_b_dw3, conv_ln_g3, conv_ln_b3,
                           mixer_w[1], i, j)
            next_srcs = ffn_srcs(i + 1, 0) if i + 1 < depth else []
            xb, ffn_w = _ffn(xb, norm_g4, *ffn_w, i, 1, next_srcs)
        outs.append(xb)
    return jnp.stack(outs)
```

```python
import functools

import jax
import jax.numpy as jnp
from jax import lax
from jax.experimental import pallas as pl
from jax.experimental.pallas import tpu as pltpu

EPS = 1e-6
FFN_RESIDUAL_SCALE = 0.5
SGU_GROUPS = 8
CHUNK = 128
CONV_WIDTH = 31
CONV_HALO = 32

ROW_TILE = 1024
FFN_ROW_TILE = 1024
FFN_SUB_ROWS = (256, 512, 256)
FF_COL_CHUNK = 256
SGU_COL_CHUNK = 256
SGU_ROW_SUB = 512
SGU_TAIL_ROWS = 256
CONV_ROW_BLOCK = 64
CONV_COL_CHUNK = 256
LANES = 128
BF16_SUBLANES = 16
VMEM_LIMIT_BYTES = 60 * 1024 * 1024

F32 = jnp.float32
BF16 = jnp.bfloat16


def _rms(x, g):
    return x * lax.rsqrt(jnp.mean(x * x, axis=-1, keepdims=True) + EPS) * g


def _layer_norm(x, g, b):
    mu = jnp.mean(x, axis=-1, keepdims=True)
    xc = x - mu
    var = jnp.mean(xc * xc, axis=-1, keepdims=True)
    return xc * lax.rsqrt(var + EPS) * g + b


def _dot(a, b):
    return jnp.dot(a, b, preferred_element_type=F32)


def _ffn_kernel(n_cast, x_ref, gpre_ref, gpost_ref, wg_ref, wu_ref, wd_ref, *refs):
    cast_in, o_ref, cast_out = refs[:n_cast], refs[n_cast], refs[n_cast + 1:2 * n_cast + 1]
    h_ref, act_ref = refs[2 * n_cast + 1:]
    d_ff = wg_ref.shape[1]
    start = 0
    for sub_rows in FFN_SUB_ROWS:
        rows = slice(start, start + sub_rows)
        start += sub_rows
        h_ref[rows, :] = _rms(x_ref[rows, :], gpre_ref[...]).astype(BF16)
        for j in range(d_ff // FF_COL_CHUNK):
            sl = slice(j * FF_COL_CHUNK, (j + 1) * FF_COL_CHUNK)
            h = h_ref[rows, :]
            gate = _dot(h, wg_ref[:, sl])
            up = _dot(h, wu_ref[:, sl])
            act_ref[rows, sl] = (jax.nn.silu(gate) * up).astype(BF16)
        y = _dot(act_ref[rows, :], wd_ref[...])
        o_ref[rows, :] = x_ref[rows, :] + FFN_RESIDUAL_SCALE * _rms(y, gpost_ref[...])
    for src_ref, dst_ref in zip(cast_in, cast_out):
        dst_ref[...] = src_ref[...].astype(BF16)


def _resident(shape, index_map):
    return pl.BlockSpec(shape, index_map, pipeline_mode=pl.Buffered(1))


def _cast_index(lead, i):
    return (*lead, i, 0)


def _ffn(x, norm_g, wg, wu, wd, layer, half, cast_srcs):
    seq, d = x.shape
    d_ff = wg.shape[-1]
    tm = FFN_ROW_TILE
    n_steps = seq // tm
    assert seq % tm == 0 and sum(FFN_SUB_ROWS) == tm and d_ff % FF_COL_CHUNK == 0
    assert all(r % BF16_SUBLANES == 0 for r in FFN_SUB_ROWS)
    g_pre, g_post = (0, 1) if half == 0 else (4, 5)
    cast_in_specs, cast_out_specs, cast_out_shapes = [], [], []
    for w, lead in cast_srcs:
        rows, cols = w.shape[-2:]
        assert rows % (n_steps * BF16_SUBLANES) == 0
        rb = rows // n_steps
        cast_in_specs.append(pl.BlockSpec(
            (None,) * len(lead) + (rb, cols), functools.partial(_cast_index, lead)))
        cast_out_specs.append(pl.BlockSpec((rb, cols), lambda i: (i, 0)))
        cast_out_shapes.append(jax.ShapeDtypeStruct((rows, cols), BF16))
    outs = pl.pallas_call(
        functools.partial(_ffn_kernel, len(cast_srcs)),
        grid=(n_steps,),
        in_specs=[
            pl.BlockSpec((tm, d), lambda i: (i, 0)),
            _resident((None, None, 1, d), lambda i: (layer, g_pre, 0, 0)),
            _resident((None, None, 1, d), lambda i: (layer, g_post, 0, 0)),
            _resident((d, d_ff), lambda i: (0, 0)),
            _resident((d, d_ff), lambda i: (0, 0)),
            _resident((d_ff, d), lambda i: (0, 0)),
        ] + cast_in_specs,
        out_specs=[pl.BlockSpec((tm, d), lambda i: (i, 0))] + cast_out_specs,
        out_shape=[jax.ShapeDtypeStruct((seq, d), F32)] + cast_out_shapes,
        scratch_shapes=[pltpu.VMEM((tm, d), BF16), pltpu.VMEM((tm, d_ff), BF16)],
        compiler_params=pltpu.CompilerParams(
            dimension_semantics=("arbitrary",), vmem_limit_bytes=VMEM_LIMIT_BYTES),
        name=f"ffn_l{layer}_h{half}",
    )(x, norm_g, norm_g, wg, wu, wd, *(w for w, _ in cast_srcs))
    return outs[0], outs[1:]


def _sgu_kernel(x_ref, gpre_ref, gpost_ref, win_ref, lng_ref, lnb_ref, ws_ref, bs_ref, wout_ref,
                o_ref, h_ref, u_ref, v_ref, gated_ref):
    tm = x_ref.shape[0]
    sgu_dim = u_ref.shape[1]
    group_dim = sgu_dim // SGU_GROUPS
    row = lax.broadcasted_iota(jnp.int32, (CHUNK, CHUNK), 0)
    col = lax.broadcasted_iota(jnp.int32, (CHUNK, CHUNK), 1)
    causal = row >= col
    for s in range(tm // SGU_ROW_SUB):
        rows = slice(s * SGU_ROW_SUB, (s + 1) * SGU_ROW_SUB)
        h_ref[rows, :] = _rms(x_ref[rows, :], gpre_ref[...]).astype(BF16)
        for j in range(sgu_dim // SGU_COL_CHUNK):
            sl = slice(j * SGU_COL_CHUNK, (j + 1) * SGU_COL_CHUNK)
            slv = slice(sgu_dim + j * SGU_COL_CHUNK, sgu_dim + (j + 1) * SGU_COL_CHUNK)
            h = h_ref[rows, :]
            v_ref[rows, sl] = jax.nn.gelu(_dot(h, win_ref[:, slv]))
        for j in range(sgu_dim // SGU_COL_CHUNK):
            sl = slice(j * SGU_COL_CHUNK, (j + 1) * SGU_COL_CHUNK)
            h = h_ref[rows, :]
            u_ref[rows, sl] = jax.nn.gelu(_dot(h, win_ref[:, sl]))
        for t in range(SGU_ROW_SUB // SGU_TAIL_ROWS):
            r0 = s * SGU_ROW_SUB + t * SGU_TAIL_ROWS
            trows = slice(r0, r0 + SGU_TAIL_ROWS)
            v_ref[trows, :] = _layer_norm(v_ref[trows, :], lng_ref[...], lnb_ref[...])
            for g in range(SGU_GROUPS):
                gsl = slice(g * group_dim, (g + 1) * group_dim)
                ws = jnp.where(causal, ws_ref[g], 0.0).astype(BF16)
                bias = bs_ref[:, g:g + 1]
                for c in range(SGU_TAIL_ROWS // CHUNK):
                    rsl = slice(r0 + c * CHUNK, r0 + (c + 1) * CHUNK)
                    mixed = _dot(ws, v_ref[rsl, gsl].astype(BF16)) + bias
                    gated_ref[rsl, gsl] = (u_ref[rsl, gsl] * mixed).astype(BF16)
            m = _dot(gated_ref[trows, :], wout_ref[...])
            o_ref[trows, :] = x_ref[trows, :] + _rms(m, gpost_ref[...])


def _sgu(x, norm_g, w_in, ln_g, ln_b, w_spatial, b_spatial_t, w_out, layer, j):
    seq, d = x.shape
    sgu_dim = w_out.shape[0]
    tm = ROW_TILE
    assert seq % tm == 0 and tm % SGU_ROW_SUB == 0
    assert SGU_ROW_SUB % SGU_TAIL_ROWS == 0 and SGU_TAIL_ROWS % CHUNK == 0
    assert sgu_dim % SGU_COL_CHUNK == 0
    return pl.pallas_call(
        _sgu_kernel,
        grid=(seq // tm,),
        in_specs=[
            pl.BlockSpec((tm, d), lambda i: (i, 0)),
            _resident((None, None, 1, d), lambda i: (layer, 2, 0, 0)),
            _resident((None, None, 1, d), lambda i: (layer, 3, 0, 0)),
            _resident((d, 2 * sgu_dim), lambda i: (0, 0)),
            _resident((None, 1, sgu_dim), lambda i: (j, 0, 0)),
            _resident((None, 1, sgu_dim), lambda i: (j, 0, 0)),
            _resident((None, SGU_GROUPS, CHUNK, CHUNK), lambda i: (j, 0, 0, 0)),
            _resident((None, CHUNK, SGU_GROUPS), lambda i: (j, 0, 0)),
            _resident((sgu_dim, d), lambda i: (0, 0)),
        ],
        out_specs=pl.BlockSpec((tm, d), lambda i: (i, 0)),
        out_shape=jax.ShapeDtypeStruct((seq, d), F32),
        scratch_shapes=[
            pltpu.VMEM((tm, d), BF16),
            pltpu.VMEM((tm, sgu_dim), F32),
            pltpu.VMEM((tm, sgu_dim), F32),
            pltpu.VMEM((tm, sgu_dim), BF16),
        ],
        compiler_params=pltpu.CompilerParams(
            dimension_semantics=("arbitrary",), vmem_limit_bytes=VMEM_LIMIT_BYTES),
        name=f"sgu_l{layer}",
    )(x, norm_g, norm_g, w_in, ln_g, ln_b, w_spatial, b_spatial_t, w_out)


def _conv_kernel(x_ref, gpre_ref, gpost_ref, pw1_ref, wdw_ref, bdw_ref, lng_ref, lnb_ref, pw2_ref,
                 o_ref, h_ref, yext_ref, conv_ref):
    tm, d = x_ref.shape
    cdim = pw2_ref.shape[0]
    ncol = cdim // LANES

    @pl.when(pl.program_id(0) == 0)
    def _():
        yext_ref[:, 0:CONV_HALO, :] = jnp.zeros((ncol, CONV_HALO, LANES), F32)

    h_ref[...] = _rms(x_ref[...], gpre_ref[...]).astype(BF16)
    first = CONV_HALO - (CONV_WIDTH - 1)
    half = CONV_ROW_BLOCK // 2
    for cc in range(cdim // CONV_COL_CHUNK):
        h = h_ref[...]
        a = _dot(h, pw1_ref[:, cc * CONV_COL_CHUNK:(cc + 1) * CONV_COL_CHUNK])
        gate = _dot(h, pw1_ref[:, cdim + cc * CONV_COL_CHUNK:cdim + (cc + 1) * CONV_COL_CHUNK])
        y = a * jax.nn.sigmoid(gate)
        for cl in range(CONV_COL_CHUNK // LANES):
            c = cc * (CONV_COL_CHUNK // LANES) + cl
            lanes = slice(c * LANES, (c + 1) * LANES)
            yext_ref[c, CONV_HALO:CONV_HALO + tm, :] = y[:, cl * LANES:(cl + 1) * LANES]
            for b in range(tm // CONV_ROW_BLOCK):
                base = b * CONV_ROW_BLOCK
                bias = jnp.broadcast_to(bdw_ref[:, lanes], (half, LANES))
                even, odd = bias, bias
                tap = yext_ref[c, pl.ds(base + first, half, stride=2), :]
                for k in range(CONV_WIDTH):
                    w = wdw_ref[k:k + 1, lanes]
                    even = even + w * tap
                    tap = yext_ref[c, pl.ds(base + first + k + 1, half, stride=2), :]
                    odd = odd + w * tap
                conv_ref[c, pl.ds(base, half, stride=2), :] = even
                conv_ref[c, pl.ds(base + 1, half, stride=2), :] = odd
            yext_ref[c, 0:CONV_HALO, :] = yext_ref[c, tm:tm + CONV_HALO, :]

    conv = jnp.concatenate([conv_ref[c] for c in range(ncol)], axis=-1)
    z = jax.nn.silu(_layer_norm(conv, lng_ref[...], lnb_ref[...])).astype(BF16)
    m = _dot(z, pw2_ref[...])
    o_ref[...] = x_ref[...] + _rms(m, gpost_ref[...])


def _conv(x, norm_g, w_pw1, w_dw, b_dw, ln_g, ln_b, w_pw2, layer, j):
    seq, d = x.shape
    cdim = w_pw2.shape[0]
    tm = ROW_TILE
    assert seq % tm == 0 and tm % CONV_ROW_BLOCK == 0 and CONV_HALO >= CONV_WIDTH - 1
    assert cdim % CONV_COL_CHUNK == 0 and CONV_COL_CHUNK % LANES == 0
    return pl.pallas_call(
        _conv_kernel,
        grid=(seq // tm,),
        in_specs=[
            pl.BlockSpec((tm, d), lambda i: (i, 0)),
            _resident((None, None, 1, d), lambda i: (layer, 2, 0, 0)),
            _resident((None, None, 1, d), lambda i: (layer, 3, 0, 0)),
            _resident((d, 2 * cdim), lambda i: (0, 0)),
            _resident((None, CONV_WIDTH, cdim), lambda i: (j, 0, 0)),
            _resident((None, 1, cdim), lambda i: (j, 0, 0)),
            _resident((None, 1, cdim), lambda i: (j, 0, 0)),
            _resident((None, 1, cdim), lambda i: (j, 0, 0)),
            _resident((cdim, d), lambda i: (0, 0)),
        ],
        out_specs=pl.BlockSpec((tm, d), lambda i: (i, 0)),
        out_shape=jax.ShapeDtypeStruct((seq, d), F32),
        scratch_shapes=[
            pltpu.VMEM((tm, d), BF16),
            pltpu.VMEM((cdim // LANES, tm + CONV_HALO, LANES), F32),
            pltpu.VMEM((cdim // LANES, tm, LANES), F32),
        ],
        compiler_params=pltpu.CompilerParams(
            dimension_semantics=("arbitrary",), vmem_limit_bytes=VMEM_LIMIT_BYTES),
        name=f"conv_l{layer}",
    )(x, norm_g, norm_g, w_pw1, w_dw, b_dw, ln_g, ln_b, w_pw2)


def kernel(x, norm_g, ff_w_gate, ff_w_up, ff_w_down, sgu_w_in, sgu_ln_g, sgu_ln_b, sgu_w_spatial,
           sgu_b_spatial, sgu_w_out, conv_w_pw1, conv_w_dw, conv_b_dw, conv_ln_g, conv_ln_b,
           conv_w_pw2):
    batch, seq, d = x.shape
    depth = norm_g.shape[0]
    norm_g4 = norm_g[:, :, None, :]
    sgu_ln_g3, sgu_ln_b3 = sgu_ln_g[:, None, :], sgu_ln_b[:, None, :]
    b_spatial_t = jnp.swapaxes(sgu_b_spatial, 1, 2)
    conv_b_dw3, conv_ln_g3, conv_ln_b3 = (a[:, None, :] for a in (conv_b_dw, conv_ln_g, conv_ln_b))

    def ffn_srcs(layer, half):
        return [(w, (layer, half)) for w in (ff_w_gate, ff_w_up, ff_w_down)]

    outs = []
    for b in range(batch):
        xb = x[b]
        ffn_w = [w[0, 0].astype(BF16) for w in (ff_w_gate, ff_w_up, ff_w_down)]
        for i in range(depth):
            j = i // 2
            if i % 2 == 0:
                mixer_srcs = [(sgu_w_in, (j,)), (sgu_w_out, (j,))]
            else:
                mixer_srcs = [(conv_w_pw1, (j,)), (conv_w_pw2, (j,))]
            xb, cast = _ffn(xb, norm_g4, *ffn_w, i, 0, mixer_srcs + ffn_srcs(i, 1))
            mixer_w, ffn_w = cast[:2], cast[2:]
            if i % 2 == 0:
                xb = _sgu(xb, norm_g4, mixer_w[0], sgu_ln_g3, sgu_ln_b3, sgu_w_spatial,
                          b_spatial_t, mixer_w[1], i, j)
            else:
                xb = _conv(xb, norm_g4, mixer_w[0], conv_w_dw, conv_b_dw3, conv_ln_g3, conv_ln_b3,
                           mixer_w[1], i, j)
            next_srcs = ffn_srcs(i + 1, 0) if i + 1 < depth else []
            xb, ffn_w = _ffn(xb, norm_g4, *ffn_w, i, 1, next_srcs)
        outs.append(xb)
    return jnp.stack(outs)
```
